```python
import jax, jax.numpy as jnp
from jax import lax
import numpy as np

D_MODEL = 1024
BATCH = 32
SEQ = 2048
DEPTH = 4

CHUNK = 64
N_MIXERS = 3
N_LAYERS_A = (DEPTH + 2) // 3
N_LAYERS_B = (DEPTH + 1) // 3
N_LAYERS_C = DEPTH // 3
EPS = 1e-6

D_FF = -(-8 * D_MODEL // (3 * 256)) * 256

SG_BLOCK = 128
SG_WIDTH = 2 * D_MODEL
SG_GROUPS = 8
SG_GROUP_DIM = SG_WIDTH // SG_GROUPS

POOL_WINDOWS = (2, 4, 8, 16)
POOL_WIDTH = D_MODEL
POOL_GROUP_DIM = POOL_WIDTH // len(POOL_WINDOWS)

RET_HEADS = D_MODEL // 256
RET_QK_DIM = D_MODEL // RET_HEADS
RET_V_DIM = 2 * D_MODEL // RET_HEADS
RET_IN_WIDTH = 2 * RET_HEADS * RET_QK_DIM + 2 * RET_HEADS * RET_V_DIM
ROPE_BASE = 10000.0

kernel_name = "hybrid_chunk_causal_gmlp_pool_retention_trunk"


def rmsnorm(x, g):
    xf = x.astype(jnp.float32)
    return xf * lax.rsqrt(jnp.mean(xf * xf, axis=-1, keepdims=True) + EPS) * g.astype(jnp.float32)


def spatial_gating_mixer(h, w_in, v_norm_g, w_s, b_s, w_out):
    B, S, _ = h.shape
    z = jax.nn.gelu(h @ w_in, approximate=False)
    u, v = jnp.split(z, 2, axis=-1)
    vf = v.astype(jnp.float32)
    mu = jnp.mean(vf, axis=-1, keepdims=True)
    var = jnp.mean(jnp.square(vf - mu), axis=-1, keepdims=True)
    v = ((vf - mu) * lax.rsqrt(var + EPS) * v_norm_g.astype(jnp.float32)).astype(h.dtype)
    nb = S // SG_BLOCK
    v = v.reshape(B, nb, SG_BLOCK, SG_GROUPS, SG_GROUP_DIM)
    cpos = jnp.arange(SG_BLOCK) // CHUNK
    mask = cpos[:, None] >= cpos[None, :]
    w = jnp.where(mask[None], w_s, 0)
    mixed = jnp.einsum('gij,bnjgc->bnigc', w, v) + b_s.T[:, :, None]
    gated = u * mixed.reshape(B, S, SG_WIDTH)
    return gated @ w_out


def multiscale_pool_mixer(h, w_in, w_grp, b_grp, scale, w_out):
    B, S, _ = h.shape
    z = h @ w_in
    cs = jnp.cumsum(z.astype(jnp.float32), axis=1)
    t = jnp.arange(S)
    zg = z.reshape(B, S, len(POOL_WINDOWS), POOL_GROUP_DIM).astype(jnp.float32)
    groups = []
    for gi, win in enumerate(POOL_WINDOWS):
        csg = cs[..., gi * POOL_GROUP_DIM:(gi + 1) * POOL_GROUP_DIM]
        prev = jnp.pad(csg, ((0, 0), (win, 0), (0, 0)))[:, :S]
        cnt = jnp.minimum(t + 1, win).astype(jnp.float32)
        groups.append((csg - prev) / cnt[None, :, None] - zg[:, :, gi])
    p = jnp.stack(groups, axis=2).astype(h.dtype)
    y = jnp.einsum('bsgc,gcd->bsgd', p, w_grp) + b_grp
    y = y.reshape(B, S, POOL_WIDTH) * scale
    return y @ w_out


def apply_rope(x, pos):
    d = x.shape[-1]
    inv = ROPE_BASE ** (-jnp.arange(0, d, 2, dtype=jnp.float32) / d)
    ang = pos[:, None] * inv[None, :]
    cos = jnp.cos(ang)[None, :, None, :]
    sin = jnp.sin(ang)[None, :, None, :]
    x1, x2 = x[..., : d // 2], x[..., d // 2:]
    return jnp.concatenate([x1 * cos - x2 * sin, x2 * cos + x1 * sin], axis=-1)


def retention_mixer(h, w_in, w_out):
    B, S, _ = h.shape
    H, dk, dv, C = RET_HEADS, RET_QK_DIM, RET_V_DIM, CHUNK
    N = S // C
    proj = h @ w_in
    qk = H * dk
    q, k, v, g = jnp.split(proj, [qk, 2 * qk, 2 * qk + H * dv], axis=-1)
    pos = jnp.arange(S, dtype=jnp.float32)
    q = apply_rope(q.reshape(B, S, H, dk).astype(jnp.float32), pos)
    k = apply_rope(k.reshape(B, S, H, dk).astype(jnp.float32), pos) * (dk ** -0.5)
    v = v.reshape(B, S, H, dv).astype(jnp.float32)
    qc = q.transpose(0, 2, 1, 3).reshape(B, H, N, C, dk)
    kc = k.transpose(0, 2, 1, 3).reshape(B, H, N, C, dk)
    vc = v.transpose(0, 2, 1, 3).reshape(B, H, N, C, dv)

    log_g = jnp.log(1.0 - jnp.exp2(-5.0 - jnp.arange(H, dtype=jnp.float32)))
    idx = jnp.arange(C, dtype=jnp.float32)
    intra_decay = jnp.exp(log_g[:, None, None] * jnp.abs(idx[:, None] - idx[None, :]))
    q_decay = jnp.exp(log_g[:, None] * (idx + 1.0))[None, :, :, None]
    k_decay = jnp.exp(log_g[:, None] * (C - 1.0 - idx))[None, :, :, None]
    chunk_decay = jnp.exp(log_g * C)[None, :, None, None]

    scores = jnp.einsum('bhncd,bhnmd->bhncm', qc, kc) * intra_decay[None, :, None]
    intra = jnp.einsum('bhncm,bhnme->bhnce', scores, vc)

    def step(state, inp):
        qi, ki, vi = inp
        inter = jnp.einsum('bhcd,bhde->bhce', qi * q_decay, state)
        state = state * chunk_decay + jnp.einsum('bhcd,bhce->bhde', ki * k_decay, vi)
        return state, inter

    xs = (jnp.moveaxis(qc, 2, 0), jnp.moveaxis(kc, 2, 0), jnp.moveaxis(vc, 2, 0))
    _, inter = lax.scan(step, jnp.zeros((B, H, dk, dv), jnp.float32), xs)
    o = (intra + jnp.moveaxis(inter, 0, 2)).reshape(B, H, S, dv)
    o = o * lax.rsqrt(jnp.mean(o * o, axis=-1, keepdims=True) + EPS)
    o = o.transpose(0, 2, 1, 3).reshape(B, S, H * dv).astype(h.dtype)
    return (jax.nn.silu(g) * o) @ w_out


def swiglu(h, w_in, w_out):
    a, b = jnp.split(h @ w_in, 2, axis=-1)
    return (jax.nn.silu(a) * b) @ w_out


def setup_inputs(seed: int = 0) -> dict:
    key = jax.random.key(seed)
    ks = jax.random.split(key, 24)
    f32 = jnp.float32
    D = D_MODEL
    nrm = lambda k, shape, s: jax.random.normal(k, shape, f32) * s
    return {
        "x": nrm(ks[0], (BATCH, SEQ, D), 1.0),
        "c": nrm(ks[1], (BATCH, D), 1.0),
        "norm_mix_g": 1.0 + nrm(ks[2], (DEPTH, D), 0.02),
        "norm_ffn_g": 1.0 + nrm(ks[3], (DEPTH, D), 0.02),
        "w_ada": nrm(ks[4], (DEPTH, D, 6 * D), 0.2 * D ** -0.5),
        "b_ada": nrm(ks[5], (DEPTH, 6 * D), 0.02),
        "w_ffn_in": nrm(ks[6], (DEPTH, D, 2 * D_FF), D ** -0.5),
        "w_ffn_out": nrm(ks[7], (DEPTH, D_FF, D), D_FF ** -0.5),
        "sg_w_in": nrm(ks[8], (N_LAYERS_A, D, 2 * SG_WIDTH), D ** -0.5),
        "sg_v_norm_g": 1.0 + nrm(ks[9], (N_LAYERS_A, SG_WIDTH), 0.02),
        "sg_w_s": nrm(ks[10], (N_LAYERS_A, SG_GROUPS, SG_BLOCK, SG_BLOCK), SG_BLOCK ** -0.5),
        "sg_b_s": 1.0 + nrm(ks[11], (N_LAYERS_A, SG_GROUPS, SG_BLOCK), 0.02),
        "sg_w_out": nrm(ks[12], (N_LAYERS_A, SG_WIDTH, D), SG_WIDTH ** -0.5),
        "pool_w_in": nrm(ks[13], (N_LAYERS_B, D, POOL_WIDTH), D ** -0.5),
        "pool_w_grp": nrm(ks[14], (N_LAYERS_B, len(POOL_WINDOWS), POOL_GROUP_DIM, POOL_GROUP_DIM), POOL_GROUP_DIM ** -0.5),
        "pool_b_grp": nrm(ks[15], (N_LAYERS_B, len(POOL_WINDOWS), POOL_GROUP_DIM), 0.02),
        "pool_scale": 1.0 + nrm(ks[16], (N_LAYERS_B, POOL_WIDTH), 0.02),
        "pool_w_out": nrm(ks[17], (N_LAYERS_B, POOL_WIDTH, D), POOL_WIDTH ** -0.5),
        "ret_w_in": nrm(ks[18], (N_LAYERS_C, D, RET_IN_WIDTH), D ** -0.5),
        "ret_w_out": nrm(ks[19], (N_LAYERS_C, RET_HEADS * RET_V_DIM, D), (RET_HEADS * RET_V_DIM) ** -0.5),
        "final_norm_g": 1.0 + nrm(ks[20], (D,), 0.02),
    }


def reference(x, c, norm_mix_g, norm_ffn_g, w_ada, b_ada, w_ffn_in, w_ffn_out,
              sg_w_in, sg_v_norm_g, sg_w_s, sg_b_s, sg_w_out,
              pool_w_in, pool_w_grp, pool_b_grp, pool_scale, pool_w_out,
              ret_w_in, ret_w_out, final_norm_g):
    dt = x.dtype
    cond = jax.nn.silu(c.astype(jnp.float32))
    for l in range(DEPTH):
        mod = cond @ w_ada[l].astype(jnp.float32) + b_ada[l].astype(jnp.float32)
        sh1, sc1, gt1, sh2, sc2, gt2 = jnp.split(mod[:, None, :], 6, axis=-1)
        h = (rmsnorm(x, norm_mix_g[l]) * (1.0 + sc1) + sh1).astype(dt)
        kind, j = l % N_MIXERS, l // N_MIXERS
        if kind == 0:
            y = spatial_gating_mixer(h, sg_w_in[j], sg_v_norm_g[j], sg_w_s[j], sg_b_s[j], sg_w_out[j])
        elif kind == 1:
            y = multiscale_pool_mixer(h, pool_w_in[j], pool_w_grp[j], pool_b_grp[j], pool_scale[j], pool_w_out[j])
        else:
            y = retention_mixer(h, ret_w_in[j], ret_w_out[j])
        x = (x + (1.0 + gt1) * y).astype(dt)
        h = (rmsnorm(x, norm_ffn_g[l]) * (1.0 + sc2) + sh2).astype(dt)
        x = (x + (1.0 + gt2) * swiglu(h, w_ffn_in[l], w_ffn_out[l])).astype(dt)
    return rmsnorm(x, final_norm_g).astype(dt)
```

```python
import functools
import math

import jax
import jax.numpy as jnp
from jax import lax
from jax.experimental import pallas as pl
from jax.experimental.pallas import tpu as pltpu

D_MODEL = 1024
DEPTH = 4
CHUNK = 64
N_MIXERS = 3
EPS = 1e-6
D_FF = 2816

SG_BLOCK = 128
SG_WIDTH = 2 * D_MODEL
SG_GROUPS = 8
SG_GROUP_DIM = SG_WIDTH // SG_GROUPS

POOL_WINDOWS = (2, 4, 8, 16)
POOL_GROUP_DIM = D_MODEL // len(POOL_WINDOWS)
POOL_HALO = 16

RET_HEADS = 4
RET_QK_DIM = 256
RET_V_DIM = 512
RET_QK = RET_HEADS * RET_QK_DIM
RET_V = RET_HEADS * RET_V_DIM
RET_IN_WIDTH = 2 * RET_QK + 2 * RET_V
ROPE_BASE = 10000.0

V7X_VMEM_BYTES = 64 * 1024 * 1024
VMEM_LIMIT_BYTES = 56 * 1024 * 1024

BF16 = jnp.bfloat16
F32 = jnp.float32

TM_FFN = 512
TM_SG = 512
TM_POOL = 512
RET_BLOCK = 256


def _dot(a, b):
    return jnp.dot(a, b, preferred_element_type=F32)


def _silu(x):
    return x * jax.nn.sigmoid(x)


def _modulated_norm(x, g, sc, sh):
    r = lax.rsqrt(jnp.mean(x * x, axis=-1, keepdims=True) + EPS)
    return x * r * g * (1.0 + sc) + sh


def _mod_kernel(c_ref, w_ref, b_ref, o_ref):
    cond = _silu(c_ref[...])
    o_ref[...] = _dot(cond.astype(BF16), w_ref[...].astype(BF16)) + b_ref[...]


def _ada_modulation(c, w_ada, b_ada):
    B = c.shape[0]
    tn = 1536
    out = pl.pallas_call(
        _mod_kernel,
        grid=(DEPTH, 6 * D_MODEL // tn),
        in_specs=[
            pl.BlockSpec((B, D_MODEL), lambda l, j: (0, 0)),
            pl.BlockSpec((None, D_MODEL, tn), lambda l, j: (l, 0, j)),
            pl.BlockSpec((None, 1, tn), lambda l, j: (l, 0, j)),
        ],
        out_specs=pl.BlockSpec((None, B, tn), lambda l, j: (l, 0, j)),
        out_shape=jax.ShapeDtypeStruct((DEPTH, B, 6 * D_MODEL), F32),
        compiler_params=pltpu.CompilerParams(
            dimension_semantics=("arbitrary", "arbitrary"), vmem_limit_bytes=VMEM_LIMIT_BYTES),
        name="ada_modulation",
    )(c, w_ada, b_ada.reshape(DEPTH, 1, 6 * D_MODEL))
    return out.reshape(DEPTH, B, 6, D_MODEL)


def _x_spec(tm):
    return pl.BlockSpec((None, tm, D_MODEL), lambda b, s: (b, s, 0))


def _mod_spec(layer):
    return pl.BlockSpec((None, None, 6, D_MODEL), lambda b, s: (layer, b, 0, 0))


def _layer_spec(arr, idx):
    shape = arr.shape[1:]
    zeros = (0,) * len(shape)
    return pl.BlockSpec((None,) + shape, lambda b, s: (idx,) + zeros, pipeline_mode=pl.Buffered(1))


def _const_spec(arr):
    zeros = (0,) * arr.ndim
    return pl.BlockSpec(arr.shape, lambda b, s: zeros, pipeline_mode=pl.Buffered(1))


def _sublayer_call(kernel, x, tm, operands, specs, scratch=(), name=None):
    B, S, _ = x.shape
    return pl.pallas_call(
        kernel,
        grid=(B, S // tm),
        in_specs=[_x_spec(tm)] + list(specs),
        out_specs=_x_spec(tm),
        out_shape=jax.ShapeDtypeStruct(x.shape, x.dtype),
        scratch_shapes=list(scratch),
        compiler_params=pltpu.CompilerParams(
            dimension_semantics=("arbitrary", "arbitrary"), vmem_limit_bytes=VMEM_LIMIT_BYTES),
        name=name,
    )(x, *operands)


def _ffn_kernel(x_ref, mod_ref, g_ref, w_in_ref, w_out_ref, fg_ref, o_ref, *, final):
    x = x_ref[...]
    h = _modulated_norm(x, g_ref[...], mod_ref[4:5, :], mod_ref[3:4, :]).astype(BF16)
    a = _dot(h, w_in_ref[:, :D_FF])
    b = _dot(h, w_in_ref[:, D_FF:])
    y = _dot((_silu(a) * b).astype(BF16), w_out_ref[...])
    out = x + (1.0 + mod_ref[5:6, :]) * y
    if final:
        out = out * lax.rsqrt(jnp.mean(out * out, axis=-1, keepdims=True) + EPS) * fg_ref[...]
    o_ref[...] = out


def _ffn_sublayer(x, mod, layer, norm_g, w_in, w_out, final_g, final):
    return _sublayer_call(
        functools.partial(_ffn_kernel, final=final), x, TM_FFN,
        (mod, norm_g, w_in, w_out, final_g),
        [_mod_spec(layer), _layer_spec(norm_g, layer), _layer_spec(w_in, layer),
         _layer_spec(w_out, layer), _const_spec(final_g)],
        name=f"ffn_{layer}")


def _sg_kernel(x_ref, mod_ref, g_ref, w_in_ref, vg_ref, ws_ref, bs_ref, w_out_ref, o_ref, gated_ref):
    tm = x_ref.shape[0]
    x = x_ref[...]
    h = _modulated_norm(x, g_ref[...], mod_ref[1:2, :], mod_ref[0:1, :]).astype(BF16)
    z = _dot(h, w_in_ref[...])
    z = 0.5 * z * (1.0 + lax.erf(z * (1.0 / math.sqrt(2.0))))
    u = z[:, :SG_WIDTH]
    v = z[:, SG_WIDTH:]
    mu = jnp.mean(v, axis=-1, keepdims=True)
    vc = v - mu
    var = jnp.mean(vc * vc, axis=-1, keepdims=True)
    v = (vc * lax.rsqrt(var + EPS) * vg_ref[...]).astype(BF16)
    qi = lax.broadcasted_iota(jnp.int32, (SG_BLOCK, SG_BLOCK), 0) // CHUNK
    kj = lax.broadcasted_iota(jnp.int32, (SG_BLOCK, SG_BLOCK), 1) // CHUNK
    mask = qi >= kj
    for grp in range(SG_GROUPS):
        w = jnp.where(mask, ws_ref[grp], 0.0).astype(BF16)
        bias = bs_ref[:, grp:grp + 1]
        cols = slice(grp * SG_GROUP_DIM, (grp + 1) * SG_GROUP_DIM)
        for blk in range(tm // SG_BLOCK):
            rows = slice(blk * SG_BLOCK, (blk + 1) * SG_BLOCK)
            mixed = _dot(w, v[rows, cols]) + bias
            gated_ref[rows, cols] = (u[rows, cols] * mixed).astype(BF16)
    y = _dot(gated_ref[...], w_out_ref[...])
    o_ref[...] = x + (1.0 + mod_ref[2:3, :]) * y


def _sg_sublayer(x, mod, layer, j, norm_g, w_in, v_norm_g, w_s, b_s_t, w_out):
    return _sublayer_call(
        _sg_kernel, x, TM_SG,
        (mod, norm_g, w_in, v_norm_g, w_s, b_s_t, w_out),
        [_mod_spec(layer), _layer_spec(norm_g, layer), _layer_spec(w_in, j), _layer_spec(v_norm_g, j),
         _layer_spec(w_s, j), _layer_spec(b_s_t, j), _layer_spec(w_out, j)],
        scratch=[pltpu.VMEM((TM_SG, SG_WIDTH), BF16)],
        name=f"spatial_gating_{layer}")


def _pool_kernel(x_ref, mod_ref, g_ref, w_in_ref, w_grp_ref, b_grp_ref, scale_ref, w_out_ref, o_ref,
                 halo_ref):
    tm = x_ref.shape[0]
    s = pl.program_id(1)

    @pl.when(s == 0)
    def _():
        halo_ref[...] = jnp.zeros_like(halo_ref)

    x = x_ref[...]
    h = _modulated_norm(x, g_ref[...], mod_ref[1:2, :], mod_ref[0:1, :]).astype(BF16)
    z = _dot(h, w_in_ref[...])
    ext = jnp.concatenate([halo_ref[...], z], axis=0)
    halo_ref[...] = z[tm - POOL_HALO:, :]
    pos = s * tm + lax.broadcasted_iota(jnp.int32, (tm, 1), 0)
    ys = []
    for gi, win in enumerate(POOL_WINDOWS):
        cols = slice(gi * POOL_GROUP_DIM, (gi + 1) * POOL_GROUP_DIM)
        acc = ext[:, cols]
        k = 1
        while k < win:
            acc = acc + pltpu.roll(acc, k, axis=0)
            k *= 2
        cnt = jnp.minimum(pos + 1, win).astype(F32)
        p = acc[POOL_HALO:, :] / cnt - z[:, cols]
        y = _dot(p.astype(BF16), w_grp_ref[gi]) + b_grp_ref[gi:gi + 1, :]
        ys.append(y * scale_ref[:, cols])
    y = jnp.concatenate(ys, axis=1).astype(BF16)
    y = _dot(y, w_out_ref[...])
    o_ref[...] = x + (1.0 + mod_ref[2:3, :]) * y


def _pool_sublayer(x, mod, layer, j, norm_g, w_in, w_grp, b_grp, scale, w_out):
    return _sublayer_call(
        _pool_kernel, x, TM_POOL,
        (mod, norm_g, w_in, w_grp, b_grp, scale, w_out),
        [_mod_spec(layer), _layer_spec(norm_g, layer), _layer_spec(w_in, j), _layer_spec(w_grp, j),
         _layer_spec(b_grp, j), _layer_spec(scale, j), _layer_spec(w_out, j)],
        scratch=[pltpu.VMEM((POOL_HALO, D_MODEL), F32)],
        name=f"pool_{layer}")


def _retention_tables(S, L):
    half = RET_QK_DIM // 2
    inv = ROPE_BASE ** (-jnp.arange(0, RET_QK_DIM, 2, dtype=F32) / RET_QK_DIM)
    ang = jnp.arange(S, dtype=F32)[:, None] * inv[None, :]
    cos, sin = jnp.cos(ang), jnp.sin(ang)
    assert cos.shape == (S, half)
    log_g = jnp.log(1.0 - jnp.exp2(-5.0 - jnp.arange(RET_HEADS, dtype=F32)))
    idx = jnp.arange(L, dtype=F32)
    chunk = jnp.arange(L) // CHUNK
    visible = chunk[None, :] <= chunk[:, None]
    decay = jnp.exp(log_g[:, None, None] * jnp.abs(idx[:, None] - idx[None, :]))
    dmat = jnp.where(visible[None], decay, 0.0)
    q_decay = jnp.exp(log_g[:, None] * (idx + 1.0))[:, :, None]
    k_decay = jnp.exp(log_g[:, None] * (L - 1.0 - idx))[:, :, None]
    s_decay = jnp.exp(log_g * L)
    s_decay = jnp.broadcast_to(s_decay[:, None, None], (RET_HEADS, 1, RET_V_DIM))
    return cos, sin, dmat, q_decay, k_decay, s_decay


def _rope(t, cos, sin):
    half = RET_QK_DIM // 2
    t1, t2 = t[:, :half], t[:, half:]
    return jnp.concatenate([t1 * cos - t2 * sin, t2 * cos + t1 * sin], axis=-1)


def _ret_kernel(x_ref, mod_ref, g_ref, w_in_ref, w_out_ref, cos_ref, sin_ref, dmat_ref, qd_ref, kd_ref,
                sd_ref, o_ref, state_ref, gated_ref):
    s = pl.program_id(1)

    @pl.when(s == 0)
    def _():
        state_ref[...] = jnp.zeros_like(state_ref)

    x = x_ref[...]
    h = _modulated_norm(x, g_ref[...], mod_ref[1:2, :], mod_ref[0:1, :]).astype(BF16)
    proj = _dot(h, w_in_ref[...])
    cos, sin = cos_ref[...], sin_ref[...]
    for hd in range(RET_HEADS):
        q = _rope(proj[:, hd * RET_QK_DIM:(hd + 1) * RET_QK_DIM], cos, sin)
        k = _rope(proj[:, RET_QK + hd * RET_QK_DIM:RET_QK + (hd + 1) * RET_QK_DIM], cos, sin)
        k = k * (RET_QK_DIM ** -0.5)
        v = proj[:, 2 * RET_QK + hd * RET_V_DIM:2 * RET_QK + (hd + 1) * RET_V_DIM].astype(BF16)
        gate = proj[:, 2 * RET_QK + RET_V + hd * RET_V_DIM:2 * RET_QK + RET_V + (hd + 1) * RET_V_DIM]
        scores = lax.dot_general(q.astype(BF16), k.astype(BF16), (((1,), (1,)), ((), ())),
                                 preferred_element_type=F32)
        scores = scores * dmat_ref[hd]
        state = state_ref[hd]
        o = _dot(scores.astype(BF16), v) + _dot((q * qd_ref[hd]).astype(BF16), state.astype(BF16))
        kt = jnp.transpose(k * kd_ref[hd]).astype(BF16)
        state_ref[hd] = state * sd_ref[hd] + _dot(kt, v)
        o = o * lax.rsqrt(jnp.mean(o * o, axis=-1, keepdims=True) + EPS)
        gated_ref[:, hd * RET_V_DIM:(hd + 1) * RET_V_DIM] = (_silu(gate) * o).astype(BF16)
    y = _dot(gated_ref[...], w_out_ref[...])
    o_ref[...] = x + (1.0 + mod_ref[2:3, :]) * y


def _ret_sublayer(x, mod, layer, j, norm_g, w_in, w_out):
    S = x.shape[1]
    L = RET_BLOCK
    cos, sin, dmat, qd, kd, sd = _retention_tables(S, L)
    rope_spec = pl.BlockSpec((L, RET_QK_DIM // 2), lambda b, s: (s, 0))
    return _sublayer_call(
        _ret_kernel, x, L,
        (mod, norm_g, w_in, w_out, cos, sin, dmat, qd, kd, sd),
        [_mod_spec(layer), _layer_spec(norm_g, layer), _layer_spec(w_in, j), _layer_spec(w_out, j),
         rope_spec, rope_spec, _const_spec(dmat), _const_spec(qd), _const_spec(kd), _const_spec(sd)],
        scratch=[pltpu.VMEM((RET_HEADS, RET_QK_DIM, RET_V_DIM), F32), pltpu.VMEM((L, RET_V), BF16)],
        name=f"retention_{layer}")


def kernel(x, c, norm_mix_g, norm_ffn_g, w_ada, b_ada, w_ffn_in, w_ffn_out, sg_w_in, sg_v_norm_g, sg_w_s,
           sg_b_s, sg_w_out, pool_w_in, pool_w_grp, pool_b_grp, pool_scale, pool_w_out, ret_w_in, ret_w_out,
           final_norm_g):
    assert x.shape[1:] == (2048, D_MODEL) and x.dtype == F32
    mod = _ada_modulation(c, w_ada, b_ada)
    row = lambda a: a[:, None, :]
    norm_mix_g, norm_ffn_g = row(norm_mix_g), row(norm_ffn_g)
    final_g = final_norm_g[None, :]
    bf = lambda a: a.astype(BF16)
    w_ffn_in, w_ffn_out = bf(w_ffn_in), bf(w_ffn_out)
    sg_w_in, sg_w_out = bf(sg_w_in), bf(sg_w_out)
    pool_w_in, pool_w_grp, pool_w_out = bf(pool_w_in), bf(pool_w_grp), bf(pool_w_out)
    ret_w_in, ret_w_out = bf(ret_w_in), bf(ret_w_out)
    sg_v_norm_g, pool_scale = row(sg_v_norm_g), row(pool_scale)
    sg_b_s_t = jnp.swapaxes(sg_b_s, 1, 2)

    for layer in range(DEPTH):
        kind, j = layer % N_MIXERS, layer // N_MIXERS
        if kind == 0:
            x = _sg_sublayer(x, mod, layer, j, norm_mix_g, sg_w_in, sg_v_norm_g, sg_w_s, sg_b_s_t, sg_w_out)
        elif kind == 1:
            x = _pool_sublayer(x, mod, layer, j, norm_mix_g, pool_w_in, pool_w_grp, pool_b_grp, pool_scale,
                               pool_w_out)
        else:
            x = _ret_sublayer(x, mod, layer, j, norm_mix_g, ret_w_in, ret_w_out)
        x = _ffn_sublayer(x, mod, layer, norm_ffn_g, w_ffn_in, w_ffn_out, final_g, final=layer == DEPTH - 1)
    return x
```
